```python
import math
import jax
import jax.numpy as jnp
from jax import lax
import numpy as np

D_MODEL = 4096
BATCH = 2
SEQ = 4096
DEPTH = 4

GRID_W = 64
CTX_LEN = 256
MIX_WIDTH = D_MODEL
NA_WIDTH = MIX_WIDTH // 2
GLA_WIDTH = MIX_WIDTH - NA_WIDTH
NA_HEAD_DIM = 128
NA_HEADS = NA_WIDTH // NA_HEAD_DIM
NA_KH = 8
NA_KW = 16
NA_QB = 16
NA_KSPAN = 32
NA_NCB = GRID_W // NA_QB
GLA_HEADS = 4
GLA_DV = GLA_WIDTH // GLA_HEADS
GLA_DK = GLA_DV // 2
GLA_KEY_WIDTH = GLA_HEADS * GLA_DK
GLA_GATE_RANK = 16
GLA_GATE_NORM = 16.0
GLA_CHUNK = 64
ROPE_BASE = 10000.0
EPS = 1e-6
NEG_INF = -1e30

IN_SIZES = (NA_WIDTH, NA_WIDTH, NA_WIDTH, NA_WIDTH,
            GLA_KEY_WIDTH, GLA_KEY_WIDTH, GLA_WIDTH, GLA_WIDTH,
            GLA_GATE_RANK, GLA_GATE_RANK)
IN_DIM = sum(IN_SIZES)
IN_SPLITS = tuple(int(s) for s in np.cumsum(IN_SIZES)[:-1])

kernel_name = 'hybrid_natten_gla_prefix_dit'


def rmsnorm(x, g):
    xf = x.astype(jnp.float32)
    y = xf * lax.rsqrt(jnp.mean(xf * xf, axis=-1, keepdims=True) + EPS)
    return (y * g.astype(jnp.float32)).astype(x.dtype)


def _rotate(x, pos):
    nf = x.shape[-1] // 2
    inv = ROPE_BASE ** (-jnp.arange(nf, dtype=jnp.float32) / nf)
    ang = pos.astype(jnp.float32)[:, None] * inv[None, :]
    cos = jnp.cos(ang)[None, :, None, :].astype(x.dtype)
    sin = jnp.sin(ang)[None, :, None, :].astype(x.dtype)
    x1, x2 = x[..., :nf], x[..., nf:]
    return jnp.concatenate([x1 * cos - x2 * sin, x1 * sin + x2 * cos], axis=-1)


def axial_rope(x, rows, cols):
    half = x.shape[-1] // 2
    return jnp.concatenate([_rotate(x[..., :half], rows), _rotate(x[..., half:], cols)], axis=-1)


def neighborhood_attention(q, k, v, k_ctx, v_ctx, rpb):
    B, L, H, Dh = q.shape
    rows = L // GRID_W
    kh = min(NA_KH, rows)
    scale = Dh ** -0.5
    qg = q.reshape(B, rows, NA_NCB, NA_QB, H, Dh)
    kg = k.reshape(B, rows, GRID_W, H, Dh)
    vg = v.reshape(B, rows, GRID_W, H, Dh)
    qcol = np.arange(GRID_W).reshape(NA_NCB, NA_QB)
    kstart = np.clip(np.arange(NA_NCB) * NA_QB - NA_KW // 2, 0, GRID_W - NA_KSPAN)
    kcol = kstart[:, None] + np.arange(NA_KSPAN)[None, :]
    cstart = np.clip(qcol - NA_KW // 2, 0, GRID_W - NA_KW)
    col_ok = (kcol[:, None, :] >= cstart[:, :, None]) & (kcol[:, None, :] < cstart[:, :, None] + NA_KW)
    dx_idx = np.clip(kcol[:, None, :] - qcol[:, :, None] + NA_KW - 1, 0, 2 * NA_KW - 2)
    rpb_x = rpb[:, :, dx_idx]
    n_loc = kh * NA_KSPAN

    def row_block(r):
        rs = jnp.clip(r - kh // 2, 0, rows - kh)
        k_blk = lax.dynamic_slice_in_dim(kg, rs, kh, axis=1)[:, :, kcol]
        v_blk = lax.dynamic_slice_in_dim(vg, rs, kh, axis=1)[:, :, kcol]
        q_r = lax.dynamic_index_in_dim(qg, r, axis=1, keepdims=False)
        s_loc = jnp.einsum('bnqhd,bynkhd->bhnqyk', q_r, k_blk).astype(jnp.float32) * scale
        dy_idx = rs + jnp.arange(kh) - r + NA_KH - 1
        bias = jnp.take(rpb_x, dy_idx, axis=1).transpose(0, 2, 3, 1, 4)
        s_loc = jnp.where(col_ok[:, :, None, :], s_loc + bias[None].astype(jnp.float32), NEG_INF)
        s_ctx = jnp.einsum('bnqhd,bchd->bhnqc', q_r, k_ctx).astype(jnp.float32) * scale
        s = jnp.concatenate([s_loc.reshape(B, H, NA_NCB, NA_QB, n_loc), s_ctx], axis=-1)
        p = jax.nn.softmax(s, axis=-1).astype(v.dtype)
        p_loc = p[..., :n_loc].reshape(B, H, NA_NCB, NA_QB, kh, NA_KSPAN)
        p_ctx = p[..., n_loc:]
        return (jnp.einsum('bhnqyk,bynkhd->bnqhd', p_loc, v_blk)
                + jnp.einsum('bhnqc,bchd->bnqhd', p_ctx, v_ctx))

    o = lax.map(row_block, jnp.arange(rows))
    return o.transpose(1, 0, 2, 3, 4, 5).reshape(B, L, H * Dh)


def context_attention(q, k, v):
    B, CL, H, Dh = q.shape
    s = jnp.einsum('bqhd,bkhd->bhqk', q, k).astype(jnp.float32) * (Dh ** -0.5)
    p = jax.nn.softmax(s, axis=-1).astype(v.dtype)
    return jnp.einsum('bhqk,bkhd->bqhd', p, v).reshape(B, CL, H * Dh)


def gla_chunk_scan(q, k, v, g, s0):
    B, L, H, DK = q.shape
    DV = v.shape[-1]
    C = GLA_CHUNK
    nc = L // C

    def chunks(t):
        return t.reshape(B, nc, C, H, t.shape[-1]).transpose(1, 0, 3, 2, 4)

    tril = jnp.tril(jnp.ones((C, C), dtype=bool))

    def step(state, inp):
        qc, kc, vc, gc = inp
        bc = jnp.cumsum(gc, axis=2)
        b_last = bc[:, :, -1]
        o_inter = jnp.einsum('bhtd,bhde->bhte', qc * jnp.exp(bc), state)
        diff = bc[:, :, :, None, :] - bc[:, :, None, :, :]
        decay = jnp.where(tril[:, :, None], jnp.exp(jnp.minimum(diff, 0.0)), 0.0)
        attn = jnp.einsum('bhtd,bhsd,bhtsd->bhts', qc, kc, decay)
        o_intra = jnp.einsum('bhts,bhse->bhte', attn, vc)
        k_dec = kc * jnp.exp(b_last[:, :, None, :] - bc)
        state = state * jnp.exp(b_last)[..., None] + jnp.einsum('bhsd,bhse->bhde', k_dec, vc)
        return state, o_inter + o_intra

    s_final, o = lax.scan(step, s0, (chunks(q), chunks(k), chunks(v), chunks(g)))
    return o.transpose(1, 0, 3, 2, 4).reshape(B, L, H, DV), s_final


def gla_final_state(k, v, g):
    b = jnp.cumsum(g, axis=1)
    return jnp.einsum('blhd,blhe->bhde', k * jnp.exp(b[:, -1:] - b), v)


def gla_bidirectional(q, k, v, g_fwd, g_bwd, q_c, k_c, v_c, g_fwd_c, g_bwd_c, with_ctx_out):
    f32 = jnp.float32
    q, k, v, q_c, k_c, v_c = (t.astype(f32) for t in (q, k, v, q_c, k_c, v_c))
    flip = lambda t: t[:, ::-1]
    B, _, H, DK = q.shape
    DV = v.shape[-1]
    if with_ctx_out:
        s0 = jnp.zeros((B, H, DK, DV), f32)
        o_cf, s_f = gla_chunk_scan(q_c, k_c, v_c, g_fwd_c, s0)
        o_cb, s_b = gla_chunk_scan(flip(q_c), flip(k_c), flip(v_c), flip(g_bwd_c), s0)
        o_ctx = o_cf + flip(o_cb)
    else:
        s_f = gla_final_state(k_c, v_c, g_fwd_c)
        s_b = gla_final_state(flip(k_c), flip(v_c), flip(g_bwd_c))
        o_ctx = None
    o_f, _ = gla_chunk_scan(q, k, v, g_fwd, s_f)
    o_b, _ = gla_chunk_scan(flip(q), flip(k), flip(v), flip(g_bwd), s_b)
    return o_f + flip(o_b), o_ctx


def gla_merge(o, gla_g, dtype):
    B, L, H, DV = o.shape
    return rmsnorm(o, gla_g).reshape(B, L, H * DV).astype(dtype)


def hybrid_layer(x, xc, mod, mod_c, norm_g, w_in, rpb, w_decay, b_decay, gla_g, w_out,
                 rows, cols, with_ctx_out):
    shift, scale, gate = jnp.split(mod, 3, axis=-1)
    shift_c, scale_c, gate_c = jnp.split(mod_c, 3, axis=-1)
    h = rmsnorm(x, norm_g) * (1.0 + scale[:, None]) + shift[:, None]
    hc = rmsnorm(xc, norm_g) * (1.0 + scale_c) + shift_c
    naq, nak, nav, nag, gq, gk, gv, gg, gf, gb = jnp.split(h @ w_in, IN_SPLITS, axis=-1)
    naq_c, nak_c, nav_c, nag_c, gq_c, gk_c, gv_c, gg_c, gf_c, gb_c = jnp.split(hc @ w_in, IN_SPLITS, axis=-1)

    nh = lambda t: t.reshape(t.shape[0], t.shape[1], NA_HEADS, NA_HEAD_DIM)
    kh = lambda t: t.reshape(t.shape[0], t.shape[1], GLA_HEADS, GLA_DK)
    vh = lambda t: t.reshape(t.shape[0], t.shape[1], GLA_HEADS, GLA_DV)

    def decay(lr, d):
        logit = (lr @ w_decay[d] + b_decay[d]).astype(jnp.float32)
        return kh(jax.nn.log_sigmoid(logit) / GLA_GATE_NORM)

    q_scale = GLA_DK ** -0.5
    na_lat = neighborhood_attention(nh(naq), nh(nak), nh(nav), nh(nak_c), nh(nav_c), rpb)
    gla_lat, gla_ctx = gla_bidirectional(
        axial_rope(kh(gq) * q_scale, rows, cols), axial_rope(kh(gk), rows, cols), vh(gv),
        decay(gf, 0), decay(gb, 1),
        kh(gq_c) * q_scale, kh(gk_c), vh(gv_c), decay(gf_c, 0), decay(gb_c, 1),
        with_ctx_out)
    y = jnp.concatenate([na_lat * jax.nn.silu(nag),
                         gla_merge(gla_lat, gla_g, x.dtype) * jax.nn.silu(gg)], axis=-1) @ w_out
    x = x + gate[:, None] * y
    if with_ctx_out:
        na_ctx = context_attention(nh(naq_c), nh(nak_c), nh(nav_c))
        yc = jnp.concatenate([na_ctx * jax.nn.silu(nag_c),
                              gla_merge(gla_ctx, gla_g, xc.dtype) * jax.nn.silu(gg_c)], axis=-1) @ w_out
        xc = xc + gate_c * yc
    return x, xc


def setup_inputs(seed: int = 0) -> dict:
    key = jax.random.key(seed)
    ks = jax.random.split(key, 14)
    f32 = jnp.float32
    D = D_MODEL
    nrm = lambda k, shape: jax.random.normal(k, shape, f32)
    return {
        'x': nrm(ks[0], (BATCH, SEQ, D)),
        'c': nrm(ks[1], (BATCH, D)),
        'ctx': nrm(ks[2], (BATCH, CTX_LEN, D)),
        'c_ctx': nrm(ks[3], (D,)),
        'ada_w': nrm(ks[4], (DEPTH, D, 3 * D)) * D ** -0.5,
        'ada_b': 0.01 * nrm(ks[5], (DEPTH, 3 * D)),
        'norm_g': 1.0 + 0.02 * nrm(ks[6], (DEPTH, D)),
        'w_in': nrm(ks[7], (DEPTH, D, IN_DIM)) * D ** -0.5,
        'na_rpb': 0.02 * nrm(ks[8], (DEPTH, NA_HEADS, 2 * NA_KH - 1, 2 * NA_KW - 1)),
        'gla_w_decay': nrm(ks[9], (DEPTH, 2, GLA_GATE_RANK, GLA_KEY_WIDTH)) * GLA_GATE_RANK ** -0.5,
        'gla_b_decay': 0.1 * nrm(ks[10], (DEPTH, 2, GLA_KEY_WIDTH)),
        'gla_norm_g': 1.0 + 0.02 * nrm(ks[11], (DEPTH, GLA_DV)),
        'w_out': nrm(ks[12], (DEPTH, MIX_WIDTH, D)) * MIX_WIDTH ** -0.5,
        'final_norm_g': 1.0 + 0.02 * nrm(ks[13], (D,)),
    }


def reference(x, c, ctx, c_ctx, ada_w, ada_b, norm_g, w_in, na_rpb, gla_w_decay, gla_b_decay,
              gla_norm_g, w_out, final_norm_g):
    L = x.shape[1]
    t = jnp.arange(L)
    rows, cols = t // GRID_W, t % GRID_W
    silu_c = jax.nn.silu(c)
    silu_cc = jax.nn.silu(c_ctx)
    xc = ctx
    for l in range(DEPTH):
        mod = silu_c @ ada_w[l] + ada_b[l]
        mod_c = silu_cc @ ada_w[l] + ada_b[l]
        x, xc = hybrid_layer(x, xc, mod, mod_c, norm_g[l], w_in[l], na_rpb[l], gla_w_decay[l],
                             gla_b_decay[l], gla_norm_g[l], w_out[l], rows, cols,
                             with_ctx_out=(l < DEPTH - 1))
    return rmsnorm(x, final_norm_g)
```

```python
import functools
import math

import jax
import jax.numpy as jnp
import numpy as np
from jax import lax
from jax.experimental import pallas as pl
from jax.experimental.pallas import tpu as pltpu

F32 = jnp.float32
BF16 = jnp.bfloat16

D_MODEL = 4096
GRID_W = 64
CTX_LEN = 256
NA_WIDTH = 2048
NA_HEAD_DIM = 128
NA_HEADS = NA_WIDTH // NA_HEAD_DIM
NA_KH = 8
NA_KW = 16
GLA_HEADS = 4
GLA_DV = 512
GLA_DK = 256
GLA_KEY_WIDTH = GLA_HEADS * GLA_DK
GLA_WIDTH = GLA_HEADS * GLA_DV
GLA_GATE_RANK = 16
GLA_GATE_NORM = 16.0
ROPE_BASE = 10000.0
EPS = 1e-6
NEG_INF = -1e30

COL_NAQ = 0
COL_NAK = COL_NAQ + NA_WIDTH
COL_NAV = COL_NAK + NA_WIDTH
COL_NAG = COL_NAV + NA_WIDTH
COL_GQ = COL_NAG + NA_WIDTH
COL_GK = COL_GQ + GLA_KEY_WIDTH
COL_GV = COL_GK + GLA_KEY_WIDTH
COL_GG = COL_GV + GLA_WIDTH
PROJ_WIDE = COL_GG + GLA_WIDTH

LANES = 128
ROW_TILE = 256
NA_HEAD_GROUP = 4
NA_ROWS_PER_STEP = ROW_TILE // GRID_W
GLA_CHUNK = 64
GLA_SUB = 16
GLA_CHUNKS_PER_STEP = ROW_TILE // GLA_CHUNK
MM_TILE_M = 1088
MM_TILE_N_IN = 1024
MM_TILE_N_OUT = 512
ADA_TILE_N = 1024
VMEM_LIMIT = 56 * 1024 * 1024


def _cparams(sem):
    return pltpu.CompilerParams(dimension_semantics=sem, vmem_limit_bytes=VMEM_LIMIT)


def _dot(a, b):
    return jnp.dot(a, b, preferred_element_type=F32)


def _dot_nt(a, b):
    return lax.dot_general(a, b, (((1,), (1,)), ((), ())), preferred_element_type=F32)


def _dot_tn(a, b):
    return lax.dot_general(a, b, (((0,), (0,)), ((), ())), preferred_element_type=F32)


def _silu(x):
    return x * (1.0 / (1.0 + jnp.exp(-x)))


def _ada_kernel(c_ref, w_ref, b_ref, o_ref):
    s = _silu(c_ref[...]).astype(BF16)
    o_ref[0] = _dot(s, w_ref[0].astype(BF16)) + b_ref[0]


def _ada_modulation(cc, ada_w, ada_b):
    depth, d, n = ada_w.shape
    return pl.pallas_call(
        _ada_kernel,
        grid=(depth, n // ADA_TILE_N),
        in_specs=[
            pl.BlockSpec((8, d), lambda l, j: (0, 0)),
            pl.BlockSpec((1, d, ADA_TILE_N), lambda l, j: (l, 0, j)),
            pl.BlockSpec((1, 1, ADA_TILE_N), lambda l, j: (l, 0, j)),
        ],
        out_specs=pl.BlockSpec((1, 8, ADA_TILE_N), lambda l, j: (l, 0, j)),
        out_shape=jax.ShapeDtypeStruct((depth, 8, n), F32),
        compiler_params=_cparams(("arbitrary", "arbitrary")),
        name="ada_modulation",
    )(cc, ada_w, ada_b.reshape(depth, 1, n))


def _prenorm_kernel(x_ref, g_ref, mod_ref, wlr_ref, h_ref, lr_ref):
    x = x_ref[0]
    y = x * lax.rsqrt(jnp.mean(x * x, axis=-1, keepdims=True) + EPS) * g_ref[...]
    h = (y * (1.0 + mod_ref[0, 1:2, :]) + mod_ref[0, 0:1, :]).astype(BF16)
    h_ref[0] = h
    lr_ref[0] = _dot(h, wlr_ref[...]).astype(BF16)


def _prenorm(xs, norm_g, mod, w_lr):
    b, t, d = xs.shape
    kind = lambda bi, ti: (jnp.where(ti == 0, 2, bi), 0, 0)
    return pl.pallas_call(
        _prenorm_kernel,
        grid=(b, t // ROW_TILE),
        in_specs=[
            pl.BlockSpec((1, ROW_TILE, d), lambda bi, ti: (bi, ti, 0)),
            pl.BlockSpec((1, d), lambda bi, ti: (0, 0)),
            pl.BlockSpec((1, 3, d), kind),
            pl.BlockSpec((d, LANES), lambda bi, ti: (0, 0)),
        ],
        out_specs=[
            pl.BlockSpec((1, ROW_TILE, d), lambda bi, ti: (bi, ti, 0)),
            pl.BlockSpec((1, ROW_TILE, LANES), lambda bi, ti: (bi, ti, 0)),
        ],
        out_shape=[jax.ShapeDtypeStruct((b, t, d), BF16), jax.ShapeDtypeStruct((b, t, LANES), BF16)],
        compiler_params=_cparams(("arbitrary", "arbitrary")),
        name="prenorm",
    )(xs, norm_g.reshape(1, d), mod, w_lr)


def _inproj_kernel(a_ref, w_ref, o_ref):
    o_ref[...] = _dot(a_ref[...], w_ref[...]).astype(o_ref.dtype)


def _inproj(h2d, w):
    m, k = h2d.shape
    n = w.shape[1]
    return pl.pallas_call(
        _inproj_kernel,
        grid=(m // MM_TILE_M, n // MM_TILE_N_IN),
        in_specs=[
            pl.BlockSpec((MM_TILE_M, k), lambda i, j: (i, 0)),
            pl.BlockSpec((k, MM_TILE_N_IN), lambda i, j: (0, j)),
        ],
        out_specs=pl.BlockSpec((MM_TILE_M, MM_TILE_N_IN), lambda i, j: (i, j)),
        out_shape=jax.ShapeDtypeStruct((m, n), BF16),
        compiler_params=_cparams(("arbitrary", "arbitrary")),
        name="inproj",
    )(h2d, w)


def _softmax_pv(parts):
    m = None
    for s, _ in parts:
        mi = jnp.max(s, axis=-1, keepdims=True)
        m = mi if m is None else jnp.maximum(m, mi)
    den = None
    acc = None
    for s, v in parts:
        e = jnp.exp(s - m)
        di = jnp.sum(e, axis=-1, keepdims=True)
        ai = _dot(e.astype(BF16), v)
        den = di if den is None else den + di
        acc = ai if acc is None else acc + ai
    return acc / den


def _na_kernel(q_ref, k_ref, v_ref, g_ref, bias_ref, o_ref):
    step = pl.program_id(2)
    scale = NA_HEAD_DIM ** -0.5
    n_rows = (k_ref.shape[1] - CTX_LEN) // GRID_W

    @pl.when(step == 0)
    def _context():
        for hh in range(NA_HEAD_GROUP):
            ln = slice(hh * NA_HEAD_DIM, (hh + 1) * NA_HEAD_DIM)
            q = q_ref[0, :, ln]
            s = _dot_nt(q, k_ref[0, 0:CTX_LEN, ln]) * scale
            o = _softmax_pv([(s, v_ref[0, 0:CTX_LEN, ln])])
            o_ref[0, :, ln] = (o * _silu(g_ref[0, :, ln].astype(F32))).astype(o_ref.dtype)

    @pl.when(step > 0)
    def _latent():
        def row_body(j, carry):
            r = (step - 1) * NA_ROWS_PER_STEP + j
            rs = jnp.clip(r - NA_KH // 2, 0, n_rows - NA_KH)
            dy0 = rs - r + NA_KH - 1
            qrow = pl.multiple_of(j * GRID_W, GRID_W)
            krow = pl.multiple_of(CTX_LEN + rs * GRID_W, GRID_W)
            for hh in range(NA_HEAD_GROUP):
                ln = slice(hh * NA_HEAD_DIM, (hh + 1) * NA_HEAD_DIM)
                q = q_ref[0, pl.ds(qrow, GRID_W), ln]
                s_loc = _dot_nt(q, k_ref[0, pl.ds(krow, NA_KH * GRID_W), ln]) * scale + bias_ref[dy0, hh]
                s_ctx = _dot_nt(q, k_ref[0, 0:CTX_LEN, ln]) * scale
                o = _softmax_pv([(s_loc, v_ref[0, pl.ds(krow, NA_KH * GRID_W), ln]),
                                 (s_ctx, v_ref[0, 0:CTX_LEN, ln])])
                gate = _silu(g_ref[0, pl.ds(qrow, GRID_W), ln].astype(F32))
                o_ref[0, pl.ds(qrow, GRID_W), ln] = (o * gate).astype(o_ref.dtype)
            return carry

        lax.fori_loop(0, NA_ROWS_PER_STEP, row_body, 0)


def _na_attention(proj, bias):
    b, t, _ = proj.shape
    gw = NA_HEAD_GROUP * NA_HEAD_DIM
    n_groups = NA_HEADS // NA_HEAD_GROUP
    col = lambda off: off // gw
    return pl.pallas_call(
        _na_kernel,
        grid=(b, n_groups, t // ROW_TILE),
        in_specs=[
            pl.BlockSpec((1, ROW_TILE, gw), lambda bi, gi, si: (bi, si, col(COL_NAQ) + gi)),
            pl.BlockSpec((1, t, gw), lambda bi, gi, si: (bi, 0, col(COL_NAK) + gi)),
            pl.BlockSpec((1, t, gw), lambda bi, gi, si: (bi, 0, col(COL_NAV) + gi)),
            pl.BlockSpec((1, ROW_TILE, gw), lambda bi, gi, si: (bi, si, col(COL_NAG) + gi)),
            pl.BlockSpec((NA_KH, NA_HEAD_GROUP, GRID_W, NA_KH * GRID_W), lambda bi, gi, si: (0, gi, 0, 0)),
        ],
        out_specs=pl.BlockSpec((1, ROW_TILE, gw), lambda bi, gi, si: (bi, si, gi)),
        out_shape=jax.ShapeDtypeStruct((b, t, NA_WIDTH), BF16),
        compiler_params=_cparams(("arbitrary", "arbitrary", "arbitrary")),
        name="na_attention",
    )(proj, proj, proj, proj, bias)


def _na_bias_table(rpb):
    qc = np.arange(GRID_W)
    kc = np.arange(GRID_W)
    cstart = np.clip(qc - NA_KW // 2, 0, GRID_W - NA_KW)
    ok = (kc[None, :] >= cstart[:, None]) & (kc[None, :] < cstart[:, None] + NA_KW)
    dx = np.clip(kc[None, :] - qc[:, None] + NA_KW - 1, 0, 2 * NA_KW - 2)
    dy = np.arange(NA_KH)[:, None] + np.arange(NA_KH)[None, :]
    t = rpb[:, dy][:, :, :, dx]
    t = jnp.where(ok[None, None, None], t.astype(F32), NEG_INF)
    t = t.transpose(1, 0, 3, 2, 4)
    return t.reshape(NA_KH, rpb.shape[0], GRID_W, NA_KH * GRID_W)


def _swap_halves(x):
    parts = [pltpu.roll(x[:, i:i + LANES], LANES // 2, 1) for i in range(0, x.shape[1], LANES)]
    return jnp.concatenate(parts, axis=1)


def _gla_chunk(q_ref, k_ref, v_ref, lr_ref, cos_ref, sin_ref, wd, bd, row0, st_ref, qs, ks, bs, dbuf, fwd):
    c, sb = GLA_CHUNK, GLA_SUB
    rows = pl.ds(row0, c)
    cos = cos_ref[rows, :]
    sin = sin_ref[rows, :]
    q = q_ref[0, rows, :].astype(F32) * (GLA_DK ** -0.5)
    k = k_ref[0, rows, :].astype(F32)
    q = q * cos + _swap_halves(q) * sin
    k = k * cos + _swap_halves(k) * sin
    v = v_ref[0, rows, :]

    logit = _dot(lr_ref[0, rows, :], wd) + bd
    g = (jnp.minimum(logit, 0.0) - jnp.log1p(jnp.exp(-jnp.abs(logit)))) * (1.0 / GLA_GATE_NORM)

    ri = lax.broadcasted_iota(jnp.int32, (c, c), 0)
    ci = lax.broadcasted_iota(jnp.int32, (c, c), 1)
    causal = (ci <= ri) if fwd else (ci >= ri)
    tri = causal.astype(BF16)
    g1 = g.astype(BF16)
    r1 = g - g1.astype(F32)
    g2 = r1.astype(BF16)
    g3 = (r1 - g2.astype(F32)).astype(BF16)
    bc = _dot(tri, g1) + _dot(tri, g2) + _dot(tri, g3)
    b_tot = bc[c - 1:c, :] if fwd else bc[0:1, :]

    qs[...] = q
    ks[...] = k
    bs[...] = bc

    state = st_ref[...]
    o = _dot_nt((q * jnp.exp(bc)).astype(BF16), state.astype(BF16))

    nb = c // sb
    col16 = lax.broadcasted_iota(jnp.int32, (sb, c), 1)
    off_rows = []
    for i in range(nb):
        if (fwd and i == 0) or (not fwd and i == nb - 1):
            off_rows.append(jnp.zeros((sb, c), F32))
            continue
        rr = i * sb - 1 if fwd else (i + 1) * sb
        ref = bs[rr:rr + 1, :]
        qt = (qs[i * sb:(i + 1) * sb, :] * jnp.exp(bs[i * sb:(i + 1) * sb, :] - ref)).astype(BF16)
        kt = (k * jnp.exp(jnp.minimum(ref - bc, 0.0))).astype(BF16)
        a = _dot_nt(qt, kt)
        keep = (col16 < i * sb) if fwd else (col16 >= (i + 1) * sb)
        off_rows.append(jnp.where(keep, a, 0.0))
    a_off = jnp.concatenate(off_rows, axis=0)

    for i in range(nb):
        qi = qs[i * sb:(i + 1) * sb, :]
        bi = bs[i * sb:(i + 1) * sb, :]
        for s in range(sb):
            rr = i * sb + s
            d = qi * ks[rr:rr + 1, :] * jnp.exp(jnp.minimum(bi - bs[rr:rr + 1, :], 0.0))
            dbuf[rr * sb:(rr + 1) * sb, :] = d.astype(BF16)
    red = _dot(dbuf[...], jnp.ones((GLA_DK, LANES), BF16))
    lane = lax.broadcasted_iota(jnp.int32, (sb, LANES), 1)
    trow = lax.broadcasted_iota(jnp.int32, (sb, LANES), 0)
    diag_rows = []
    for i in range(nb):
        acc = jnp.zeros((sb, LANES), F32)
        for s in range(sb):
            rr = i * sb + s
            acc = jnp.where(lane == rr, red[rr * sb:(rr + 1) * sb, :], acc)
        keep = (lane <= trow + i * sb) if fwd else (lane >= trow + i * sb)
        diag_rows.append(jnp.where(keep, acc, 0.0))
    a_diag = jnp.concatenate(diag_rows, axis=0)[:, :c]

    o = o + _dot((a_off + a_diag).astype(BF16), v)

    k_dec = (k * jnp.exp(b_tot - bc)).astype(BF16)
    st_ref[...] = state * jnp.exp(b_tot) + _dot_tn(v, k_dec)
    return o


def _gla_kernel(qf, kf, vf, lrf, cosf, sinf, qb, kb, vb, lrb, cosb, sinb, wd_ref, bd_ref,
                of_ref, ob_ref, stf, stb, qsf, ksf, bsf, dbf, qsb, ksb, bsb, dbb):
    @pl.when(pl.program_id(2) == 0)
    def _init():
        stf[...] = jnp.zeros_like(stf)
        stb[...] = jnp.zeros_like(stb)

    def body(ci, carry):
        rf = pl.multiple_of(ci * GLA_CHUNK, GLA_CHUNK)
        rb = pl.multiple_of((GLA_CHUNKS_PER_STEP - 1 - ci) * GLA_CHUNK, GLA_CHUNK)
        o_f = _gla_chunk(qf, kf, vf, lrf, cosf, sinf, wd_ref[0], bd_ref[0], rf, stf, qsf, ksf, bsf, dbf, True)
        of_ref[0, pl.ds(rf, GLA_CHUNK), :] = o_f
        o_b = _gla_chunk(qb, kb, vb, lrb, cosb, sinb, wd_ref[1], bd_ref[1], rb, stb, qsb, ksb, bsb, dbb, False)
        ob_ref[0, pl.ds(rb, GLA_CHUNK), :] = o_b
        return carry

    lax.fori_loop(0, GLA_CHUNKS_PER_STEP, body, 0)


def _gla_scan(proj, lr, cos, sin, wd, bd):
    b, t, _ = proj.shape
    n_steps = t // ROW_TILE
    fwd = lambda si: si
    bwd = lambda si: jnp.where(si == 0, 0, n_steps - si)

    def specs(tile):
        return [
            pl.BlockSpec((1, ROW_TILE, GLA_DK), lambda bi, hi, si: (bi, tile(si), COL_GQ // GLA_DK + hi)),
            pl.BlockSpec((1, ROW_TILE, GLA_DK), lambda bi, hi, si: (bi, tile(si), COL_GK // GLA_DK + hi)),
            pl.BlockSpec((1, ROW_TILE, GLA_DV), lambda bi, hi, si: (bi, tile(si), COL_GV // GLA_DV + hi)),
            pl.BlockSpec((1, ROW_TILE, LANES), lambda bi, hi, si: (bi, tile(si), 0)),
            pl.BlockSpec((ROW_TILE, GLA_DK), lambda bi, hi, si: (tile(si), 0)),
            pl.BlockSpec((ROW_TILE, GLA_DK), lambda bi, hi, si: (tile(si), 0)),
        ]

    out_spec = lambda tile: pl.BlockSpec((1, ROW_TILE, GLA_DV), lambda bi, hi, si: (bi, tile(si), hi))
    chunk_scratch = [
        pltpu.VMEM((GLA_CHUNK, GLA_DK), F32), pltpu.VMEM((GLA_CHUNK, GLA_DK), F32),
        pltpu.VMEM((GLA_CHUNK, GLA_DK), F32), pltpu.VMEM((GLA_CHUNK * GLA_SUB, GLA_DK), BF16),
    ]
    return pl.pallas_call(
        _gla_kernel,
        grid=(b, GLA_HEADS, n_steps),
        in_specs=specs(fwd) + specs(bwd) + [
            pl.BlockSpec((2, LANES, GLA_DK), lambda bi, hi, si: (0, 0, hi)),
            pl.BlockSpec((2, 1, GLA_DK), lambda bi, hi, si: (0, 0, hi)),
        ],
        out_specs=[out_spec(fwd), out_spec(bwd)],
        out_shape=[jax.ShapeDtypeStruct((b, t, GLA_WIDTH), F32)] * 2,
        scratch_shapes=[pltpu.VMEM((GLA_DV, GLA_DK), F32), pltpu.VMEM((GLA_DV, GLA_DK), F32)]
        + chunk_scratch + chunk_scratch,
        compiler_params=_cparams(("arbitrary", "arbitrary", "arbitrary")),
        name="gla_scan",
    )(proj, proj, proj, lr, cos, sin, proj, proj, proj, lr, cos, sin, wd, bd)


def _rope_tables(n_latent):
    t = jnp.arange(n_latent)
    nf = GLA_DK // 4
    inv = ROPE_BASE ** (-jnp.arange(nf, dtype=F32) / nf)

    def cs(pos):
        ang = pos.astype(F32)[:, None] * inv[None, :]
        c, s = jnp.cos(ang), jnp.sin(ang)
        return jnp.concatenate([c, c], axis=1), jnp.concatenate([-s, s], axis=1)

    cr, sr = cs(t // GRID_W)
    cc, sc = cs(t % GRID_W)
    cos = jnp.concatenate([cr, cc], axis=1)
    sin = jnp.concatenate([sr, sc], axis=1)
    cos = jnp.concatenate([jnp.ones((CTX_LEN, GLA_DK), F32), cos], axis=0)
    sin = jnp.concatenate([jnp.zeros((CTX_LEN, GLA_DK), F32), sin], axis=0)
    return cos, sin


def _merge_kernel(of_ref, ob_ref, gg_ref, g_ref, o_ref):
    o = of_ref[0] + ob_ref[0]
    y = o * lax.rsqrt(jnp.mean(o * o, axis=-1, keepdims=True) + EPS) * g_ref[...]
    o_ref[0] = (y * _silu(gg_ref[0].astype(F32))).astype(o_ref.dtype)


def _gla_merge(o_f, o_b, proj, gla_g):
    b, t, _ = o_f.shape
    tile = lambda bi, ti, hi: (bi, ti, hi)
    return pl.pallas_call(
        _merge_kernel,
        grid=(b, t // ROW_TILE, GLA_HEADS),
        in_specs=[
            pl.BlockSpec((1, ROW_TILE, GLA_DV), tile),
            pl.BlockSpec((1, ROW_TILE, GLA_DV), tile),
            pl.BlockSpec((1, ROW_TILE, GLA_DV), lambda bi, ti, hi: (bi, ti, COL_GG // GLA_DV + hi)),
            pl.BlockSpec((1, GLA_DV), lambda bi, ti, hi: (0, 0)),
        ],
        out_specs=pl.BlockSpec((1, ROW_TILE, GLA_DV), tile),
        out_shape=jax.ShapeDtypeStruct((b, t, GLA_WIDTH), BF16),
        compiler_params=_cparams(("arbitrary", "arbitrary", "arbitrary")),
        name="gla_merge",
    )(o_f, o_b, proj, gla_g.reshape(1, GLA_DV))


MM_TILES_PER_BATCH = (CTX_LEN + GRID_W * GRID_W) // MM_TILE_M


def _outproj_kernel(na_ref, gl_ref, wa_ref, wb_ref, x_ref, gate_ref, o_ref):
    y = _dot(na_ref[...], wa_ref[...]) + _dot(gl_ref[...], wb_ref[...])
    row = lax.broadcasted_iota(jnp.int32, y.shape, 0)
    first_tile = (pl.program_id(0) % MM_TILES_PER_BATCH) == 0
    is_ctx = jnp.logical_and(first_tile, row < CTX_LEN)
    gate = jnp.where(is_ctx, gate_ref[0, 1:2, :], gate_ref[0, 0:1, :])
    o_ref[...] = x_ref[...] + gate * y


def _outproj(na, gl, w_out, xs2d, gates):
    m, d = xs2d.shape
    half = na.shape[1]
    per_batch = MM_TILES_PER_BATCH
    return pl.pallas_call(
        _outproj_kernel,
        grid=(m // MM_TILE_M, d // MM_TILE_N_OUT),
        in_specs=[
            pl.BlockSpec((MM_TILE_M, half), lambda i, j: (i, 0)),
            pl.BlockSpec((MM_TILE_M, half), lambda i, j: (i, 0)),
            pl.BlockSpec((half, MM_TILE_N_OUT), lambda i, j: (0, j)),
            pl.BlockSpec((half, MM_TILE_N_OUT), lambda i, j: (1, j)),
            pl.BlockSpec((MM_TILE_M, MM_TILE_N_OUT), lambda i, j: (i, j)),
            pl.BlockSpec((1, 2, MM_TILE_N_OUT), lambda i, j: (i // per_batch, 0, j)),
        ],
        out_specs=pl.BlockSpec((MM_TILE_M, MM_TILE_N_OUT), lambda i, j: (i, j)),
        out_shape=jax.ShapeDtypeStruct((m, d), F32),
        input_output_aliases={4: 0},
        compiler_params=_cparams(("arbitrary", "arbitrary")),
        name="outproj_residual",
    )(na, gl, w_out, w_out, xs2d, gates)


def _final_norm_kernel(x_ref, g_ref, o_ref):
    x = x_ref[0]
    o_ref[0] = x * lax.rsqrt(jnp.mean(x * x, axis=-1, keepdims=True) + EPS) * g_ref[...]


def _final_norm(xs, g, n_latent):
    b, _, d = xs.shape
    skip = CTX_LEN // ROW_TILE
    return pl.pallas_call(
        _final_norm_kernel,
        grid=(b, n_latent // ROW_TILE),
        in_specs=[
            pl.BlockSpec((1, ROW_TILE, d), lambda bi, ti: (bi, ti + skip, 0)),
            pl.BlockSpec((1, d), lambda bi, ti: (0, 0)),
        ],
        out_specs=pl.BlockSpec((1, ROW_TILE, d), lambda bi, ti: (bi, ti, 0)),
        out_shape=jax.ShapeDtypeStruct((b, n_latent, d), F32),
        compiler_params=_cparams(("arbitrary", "arbitrary")),
        name="final_norm",
    )(xs, g.reshape(1, d))


def kernel(x, c, ctx, c_ctx, ada_w, ada_b, norm_g, w_in, na_rpb, gla_w_decay, gla_b_decay, gla_norm_g,
           w_out, final_norm_g):
    b, n_latent, d = x.shape
    depth = ada_w.shape[0]
    assert (d, n_latent, ctx.shape[1]) == (D_MODEL, GRID_W * GRID_W, CTX_LEN) and b <= 2
    t = CTX_LEN + n_latent

    cc = jnp.zeros((8, d), F32).at[:b].set(c).at[2].set(c_ctx)
    mods = _ada_modulation(cc, ada_w, ada_b).reshape(depth, 8, 3, d)

    cos, sin = _rope_tables(n_latent)
    xs = jnp.concatenate([ctx, x], axis=1)

    for l in range(depth):
        w_wide = w_in[l, :, :PROJ_WIDE].astype(BF16)
        w_lr = jnp.pad(w_in[l, :, PROJ_WIDE:], ((0, 0), (0, LANES - 2 * GLA_GATE_RANK))).astype(BF16)
        wd = jnp.zeros((2, LANES, GLA_KEY_WIDTH), F32)
        wd = wd.at[0, :GLA_GATE_RANK].set(gla_w_decay[l, 0])
        wd = wd.at[1, GLA_GATE_RANK:2 * GLA_GATE_RANK].set(gla_w_decay[l, 1]).astype(BF16)
        bd = gla_b_decay[l].reshape(2, 1, GLA_KEY_WIDTH)
        gates = jnp.stack([mods[l, :b, 2], jnp.broadcast_to(mods[l, 2, 2], (b, d))], axis=1)

        h, lr = _prenorm(xs, norm_g[l], mods[l], w_lr)
        proj = _inproj(h.reshape(b * t, d), w_wide).reshape(b, t, PROJ_WIDE)
        na = _na_attention(proj, _na_bias_table(na_rpb[l]))
        o_f, o_b = _gla_scan(proj, lr, cos, sin, wd, bd)
        gl = _gla_merge(o_f, o_b, proj, gla_norm_g[l])
        xs = _outproj(na.reshape(b * t, NA_WIDTH), gl.reshape(b * t, GLA_WIDTH), w_out[l].astype(BF16),
                      xs.reshape(b * t, d), gates).reshape(b, t, d)

    return _final_norm(xs, final_norm_g, n_latent)
```

```python
import math

import jax
import jax.numpy as jnp
import numpy as np
from jax import lax
from jax.experimental import pallas as pl
from jax.experimental.pallas import tpu as pltpu

F32 = jnp.float32
BF16 = jnp.bfloat16

D_MODEL = 4096
GRID_W = 64
CTX_LEN = 256
NA_WIDTH = 2048
NA_HEAD_DIM = 128
NA_HEADS = NA_WIDTH // NA_HEAD_DIM
NA_KH = 8
NA_KW = 16
GLA_HEADS = 4
GLA_DV = 512
GLA_DK = 256
GLA_KEY_WIDTH = GLA_HEADS * GLA_DK
GLA_WIDTH = GLA_HEADS * GLA_DV
GLA_GATE_RANK = 16
GLA_GATE_NORM = 16.0
ROPE_BASE = 10000.0
EPS = 1e-6
NEG_INF = -1e30
LOG2E = math.log2(math.e)

COL_NAQ = 0
COL_NAK = COL_NAQ + NA_WIDTH
COL_NAV = COL_NAK + NA_WIDTH
COL_NAG = COL_NAV + NA_WIDTH
COL_GQ = COL_NAG + NA_WIDTH
COL_GK = COL_GQ + GLA_KEY_WIDTH
COL_GV = COL_GK + GLA_KEY_WIDTH
COL_GG = COL_GV + GLA_WIDTH
PROJ_WIDE = COL_GG + GLA_WIDTH

LANES = 128
SUBLANES = 8
ROW_TILE = 256
NA_HEAD_GROUP = 4
NA_ROWS_PER_STEP = ROW_TILE // GRID_W
GLA_CHUNK = 64
GLA_SUB = SUBLANES
GLA_LEVELS = tuple(GLA_CHUNK >> s for s in range(1, (GLA_CHUNK // GLA_SUB).bit_length()))
GLA_CHUNKS_PER_STEP = ROW_TILE // GLA_CHUNK
MM_TILE_M = 1088
MM_TILE_N = 512
MM_TILES_PER_BATCH = (CTX_LEN + GRID_W * GRID_W) // MM_TILE_M
ADA_TILE_N = 1024
VMEM_LIMIT = 56 * 1024 * 1024


def _cparams(sem):
    return pltpu.CompilerParams(dimension_semantics=sem, vmem_limit_bytes=VMEM_LIMIT)


def _dot(a, b):
    return jnp.dot(a, b, preferred_element_type=F32)


def _dot_nt(a, b):
    return lax.dot_general(a, b, (((1,), (1,)), ((), ())), preferred_element_type=F32)


def _dot_tn(a, b):
    return lax.dot_general(a, b, (((0,), (0,)), ((), ())), preferred_element_type=F32)


def _silu(x):
    return x * (1.0 / (1.0 + jnp.exp(-x)))


def _swap_halves(x):
    parts = [pltpu.roll(x[:, i:i + LANES], LANES // 2, 1) for i in range(0, x.shape[1], LANES)]
    return jnp.concatenate(parts, axis=1)


def _ada_kernel(c_ref, w_ref, b_ref, o_ref):
    s = _silu(c_ref[...]).astype(BF16)
    o_ref[0] = _dot(s, w_ref[0].astype(BF16)) + b_ref[0]


def _ada_modulation(cc, ada_w, ada_b):
    depth, d, n = ada_w.shape
    return pl.pallas_call(
        _ada_kernel,
        grid=(depth, n // ADA_TILE_N),
        in_specs=[
            pl.BlockSpec((8, d), lambda l, j: (0, 0)),
            pl.BlockSpec((1, d, ADA_TILE_N), lambda l, j: (l, 0, j)),
            pl.BlockSpec((1, 1, ADA_TILE_N), lambda l, j: (l, 0, j)),
        ],
        out_specs=pl.BlockSpec((1, 8, ADA_TILE_N), lambda l, j: (l, 0, j)),
        out_shape=jax.ShapeDtypeStruct((depth, 8, n), F32),
        compiler_params=_cparams(("arbitrary", "arbitrary")),
        name="ada_modulation",
    )(cc, ada_w, ada_b.reshape(depth, 1, n))


def _prenorm_kernel(x_ref, g_ref, mod_ref, wlr_ref, h_ref, lr_ref):
    x = x_ref[0]
    y = x * lax.rsqrt(jnp.mean(x * x, axis=-1, keepdims=True) + EPS) * g_ref[...]
    h = (y * (1.0 + mod_ref[0, 1:2, :]) + mod_ref[0, 0:1, :]).astype(BF16)
    h_ref[0] = h
    lr_ref[0] = _dot(h, wlr_ref[...]).astype(BF16)


def _prenorm(xs, norm_g, mod, w_lr):
    b, t, d = xs.shape
    kind = lambda bi, ti: (jnp.where(ti == 0, 2, bi), 0, 0)
    return pl.pallas_call(
        _prenorm_kernel,
        grid=(b, t // ROW_TILE),
        in_specs=[
            pl.BlockSpec((1, ROW_TILE, d), lambda bi, ti: (bi, ti, 0)),
            pl.BlockSpec((1, d), lambda bi, ti: (0, 0)),
            pl.BlockSpec((1, 3, d), kind),
            pl.BlockSpec((d, LANES), lambda bi, ti: (0, 0)),
        ],
        out_specs=[
            pl.BlockSpec((1, ROW_TILE, d), lambda bi, ti: (bi, ti, 0)),
            pl.BlockSpec((1, ROW_TILE, LANES), lambda bi, ti: (bi, ti, 0)),
        ],
        out_shape=[jax.ShapeDtypeStruct((b, t, d), BF16), jax.ShapeDtypeStruct((b, t, LANES), BF16)],
        compiler_params=_cparams(("arbitrary", "arbitrary")),
        name="prenorm",
    )(xs, norm_g.reshape(1, d), mod, w_lr)


ROPE_TILE_LO = COL_GQ // MM_TILE_N
ROPE_TILE_MID = COL_GK // MM_TILE_N
ROPE_TILE_HI = COL_GV // MM_TILE_N


def _inproj_kernel(a_ref, w_ref, cos_ref, sin_ref, o_ref):
    j = pl.program_id(1)
    acc = _dot(a_ref[...], w_ref[0].astype(BF16))
    is_rope = jnp.logical_and(j >= ROPE_TILE_LO, j < ROPE_TILE_HI)

    @pl.when(jnp.logical_not(is_rope))
    def _plain():
        o_ref[...] = acc.astype(o_ref.dtype)

    @pl.when(is_rope)
    def _rotary():
        x = acc * jnp.where(j < ROPE_TILE_MID, GLA_DK ** -0.5, 1.0)
        reps = MM_TILE_N // GLA_DK
        cos = jnp.concatenate([cos_ref[...]] * reps, axis=1)
        sin = jnp.concatenate([sin_ref[...]] * reps, axis=1)
        o_ref[...] = (x * cos + _swap_halves(x) * sin).astype(o_ref.dtype)


def _inproj(h2d, w_in, layer, cos, sin):
    m, k = h2d.shape
    return pl.pallas_call(
        _inproj_kernel,
        grid=(m // MM_TILE_M, PROJ_WIDE // MM_TILE_N),
        in_specs=[
            pl.BlockSpec((MM_TILE_M, k), lambda i, j: (i, 0)),
            pl.BlockSpec((1, k, MM_TILE_N), lambda i, j: (layer, 0, j)),
            pl.BlockSpec((MM_TILE_M, GLA_DK), lambda i, j: (i % MM_TILES_PER_BATCH, 0)),
            pl.BlockSpec((MM_TILE_M, GLA_DK), lambda i, j: (i % MM_TILES_PER_BATCH, 0)),
        ],
        out_specs=pl.BlockSpec((MM_TILE_M, MM_TILE_N), lambda i, j: (i, j)),
        out_shape=jax.ShapeDtypeStruct((m, PROJ_WIDE), BF16),
        compiler_params=_cparams(("arbitrary", "arbitrary")),
        name="inproj",
    )(h2d, w_in, cos, sin)


def _rope_tables(n_latent):
    t = jnp.arange(n_latent)
    nf = GLA_DK // 4
    inv = ROPE_BASE ** (-jnp.arange(nf, dtype=F32) / nf)

    def cs(pos):
        ang = pos.astype(F32)[:, None] * inv[None, :]
        c, s = jnp.cos(ang), jnp.sin(ang)
        return jnp.concatenate([c, c], axis=1), jnp.concatenate([-s, s], axis=1)

    cr, sr = cs(t // GRID_W)
    cc, sc = cs(t % GRID_W)
    cos = jnp.concatenate([cr, cc], axis=1)
    sin = jnp.concatenate([sr, sc], axis=1)
    cos = jnp.concatenate([jnp.ones((CTX_LEN, GLA_DK), F32), cos], axis=0)
    sin = jnp.concatenate([jnp.zeros((CTX_LEN, GLA_DK), F32), sin], axis=0)
    return cos, sin


def _softmax_pv(s_ctx, v_ctx, s_loc=None, v_loc=None):
    m = jnp.max(s_ctx, axis=-1, keepdims=True)
    if s_loc is not None:
        s_all = jnp.concatenate(s_loc, axis=0)
        m = jnp.maximum(m, jnp.max(s_all, axis=-1, keepdims=True))
        e_loc = jnp.exp(s_all - m)
    e_ctx = jnp.exp(s_ctx - m)
    den = jnp.sum(e_ctx, axis=-1, keepdims=True)
    acc = _dot(e_ctx.astype(BF16), v_ctx)
    if s_loc is not None:
        den = den + jnp.sum(e_loc, axis=-1, keepdims=True)
        e_loc = e_loc.astype(BF16)
        n = s_loc[0].shape[0]
        acc = acc + jnp.concatenate([_dot(e_loc[i * n:(i + 1) * n], v) for i, v in enumerate(v_loc)], axis=0)
    return acc / den


def _na_kernel(q_ref, k_ref, v_ref, g_ref, bias_ref, o_ref):
    step = pl.program_id(2)
    scale = NA_HEAD_DIM ** -0.5
    n_rows = (k_ref.shape[1] - CTX_LEN) // GRID_W
    heads = [slice(hh * NA_HEAD_DIM, (hh + 1) * NA_HEAD_DIM) for hh in range(NA_HEAD_GROUP)]
    ctx_rows = slice(0, CTX_LEN)

    def finish(o, ln):
        return (o * _silu(g_ref[0, :, ln].astype(F32))).astype(o_ref.dtype)

    @pl.when(step == 0)
    def _context():
        outs = []
        for ln in heads:
            s = _dot_nt(q_ref[0, :, ln], k_ref[0, ctx_rows, ln]) * scale
            outs.append(finish(_softmax_pv(s, v_ref[0, ctx_rows, ln]), ln))
        o_ref[0] = jnp.concatenate(outs, axis=1)

    @pl.when(step > 0)
    def _latent():
        krows, variants = [], []
        for j in range(NA_ROWS_PER_STEP):
            r = (step - 1) * NA_ROWS_PER_STEP + j
            rs = jnp.clip(r - NA_KH // 2, 0, n_rows - NA_KH)
            variants.append(rs - r + NA_KH - 1)
            krows.append(pl.ds(pl.multiple_of(CTX_LEN + rs * GRID_W, GRID_W), NA_KH * GRID_W))

        def scores(hh):
            ln = heads[hh]
            s_ctx = _dot_nt(q_ref[0, :, ln], k_ref[0, ctx_rows, ln]) * scale
            s_loc = [_dot_nt(q_ref[0, j * GRID_W:(j + 1) * GRID_W, ln], k_ref[0, krows[j], ln]) * scale
                     + bias_ref[variants[j], hh] for j in range(NA_ROWS_PER_STEP)]
            return s_ctx, s_loc

        outs = []
        nxt = scores(0)
        for hh, ln in enumerate(heads):
            s_ctx, s_loc = nxt
            if hh + 1 < NA_HEAD_GROUP:
                nxt = scores(hh + 1)
            o = _softmax_pv(s_ctx, v_ref[0, ctx_rows, ln], s_loc, [v_ref[0, kr, ln] for kr in krows])
            outs.append(finish(o, ln))
        o_ref[0] = jnp.concatenate(outs, axis=1)


def _na_attention(proj, bias):
    b, t, _ = proj.shape
    gw = NA_HEAD_GROUP * NA_HEAD_DIM
    n_groups = NA_HEADS // NA_HEAD_GROUP
    col = lambda off: off // gw
    return pl.pallas_call(
        _na_kernel,
        grid=(b, n_groups, t // ROW_TILE),
        in_specs=[
            pl.BlockSpec((1, ROW_TILE, gw), lambda bi, gi, si: (bi, si, col(COL_NAQ) + gi)),
            pl.BlockSpec((1, t, gw), lambda bi, gi, si: (bi, 0, col(COL_NAK) + gi)),
            pl.BlockSpec((1, t, gw), lambda bi, gi, si: (bi, 0, col(COL_NAV) + gi)),
            pl.BlockSpec((1, ROW_TILE, gw), lambda bi, gi, si: (bi, si, col(COL_NAG) + gi)),
            pl.BlockSpec((NA_KH, NA_HEAD_GROUP, GRID_W, NA_KH * GRID_W), lambda bi, gi, si: (0, gi, 0, 0)),
        ],
        out_specs=pl.BlockSpec((1, ROW_TILE, gw), lambda bi, gi, si: (bi, si, gi)),
        out_shape=jax.ShapeDtypeStruct((b, t, NA_WIDTH), BF16),
        compiler_params=_cparams(("arbitrary", "arbitrary", "arbitrary")),
        name="na_attention",
    )(proj, proj, proj, proj, bias)


def _na_bias_table(rpb):
    qc = np.arange(GRID_W)
    kc = np.arange(GRID_W)
    cstart = np.clip(qc - NA_KW // 2, 0, GRID_W - NA_KW)
    ok = (kc[None, :] >= cstart[:, None]) & (kc[None, :] < cstart[:, None] + NA_KW)
    dx = np.clip(kc[None, :] - qc[:, None] + NA_KW - 1, 0, 2 * NA_KW - 2)
    dy = np.arange(NA_KH)[:, None] + np.arange(NA_KH)[None, :]
    t = rpb[:, dy][:, :, :, dx]
    t = jnp.where(ok[None, None, None], t.astype(F32), NEG_INF)
    t = t.transpose(1, 0, 3, 2, 4)
    return t.reshape(NA_KH, rpb.shape[0], GRID_W, NA_KH * GRID_W)


def _gla_masks():
    r = np.arange(ROW_TILE)[:, None]
    c = np.arange(ROW_TILE)[None, :]
    same_chunk = (r // GLA_CHUNK) == (c // GLA_CHUNK)
    tri = np.stack([same_chunk & (c <= r), same_chunk & (c >= r)])
    lvl = []
    for d in range(2):
        t, s = (r, c) if d == 0 else (c, r)
        lvl.append(np.stack([((t // w) == (s // w) + 1) & ((t // w) % 2 == 1) for w in GLA_LEVELS]))
    return jnp.asarray(tri, BF16), jnp.asarray(np.stack(lvl), F32)


def _gla_prepare(q_ref, k_ref, v_ref, lr_ref, wd, bd, tri, lvl_ref, ks, bs, dbuf, fwd):
    c, sb, n = GLA_CHUNK, GLA_SUB, GLA_CHUNKS_PER_STEP
    q = q_ref[0].astype(F32)
    k = k_ref[0].astype(F32)
    v = v_ref[0]

    logit = _dot(lr_ref[0], wd) + bd
    g = (jnp.minimum(logit, 0.0) - jnp.log1p(jnp.exp(-jnp.abs(logit)))) * (LOG2E / GLA_GATE_NORM)
    g1 = g.astype(BF16)
    g2 = (g - g1.astype(F32)).astype(BF16)
    bc = _dot(tri, g1) + _dot(tri, g2)
    ks[...] = k
    bs[...] = bc

    def boundary_rows(group, offset):
        parts = [jnp.broadcast_to(bs[r0 + offset:r0 + offset + 1, :], (group, GLA_DK))
                 for r0 in range(0, n * c, group)]
        return jnp.concatenate(parts, axis=0)

    tot_row = c - 1 if fwd else 0
    b_tot = boundary_rows(c, tot_row)
    q_in = (q * jnp.exp2(bc)).astype(BF16)
    k_dec = (k * jnp.exp2(b_tot - bc)).astype(BF16)
    e_tot = [jnp.exp2(bs[ci * c + tot_row:ci * c + tot_row + 1, :]) for ci in range(n)]

    a = None
    for li, w in enumerate(GLA_LEVELS):
        ref = boundary_rows(2 * w, w - 1 if fwd else w)
        qt = (q * jnp.exp2(jnp.minimum(bc - ref, 0.0))).astype(BF16)
        kt = (k * jnp.exp2(jnp.minimum(ref - bc, 0.0))).astype(BF16)
        al = _dot_nt(qt, kt) * lvl_ref[li]
        a = al if a is None else a + al

    for i in range(n * c // sb):
        rows = slice(i * sb, (i + 1) * sb)
        qi, ki, bi = q[rows], k[rows], bc[rows]
        for s in range(0, sb, 2):
            pair = []
            for s1 in (s, s + 1):
                kb = jnp.broadcast_to(ki[s1:s1 + 1, :], (sb, GLA_DK))
                bb = jnp.broadcast_to(bi[s1:s1 + 1, :], (sb, GLA_DK))
                pair.append(qi * kb * jnp.exp2(jnp.minimum(bi - bb, 0.0)))
            r0 = (i * sb + s) * sb
            dbuf[r0:r0 + 2 * sb, :] = jnp.concatenate(pair, axis=0).astype(BF16)
    red = _dot(dbuf[...], jnp.ones((GLA_DK, LANES), BF16))
    lane = lax.broadcasted_iota(jnp.int32, (sb, LANES), 1)
    trow = lax.broadcasted_iota(jnp.int32, (sb, LANES), 0)
    zero = jnp.zeros((sb, LANES), F32)
    diag_rows = []
    for i in range(n * c // sb):
        col0 = (i * sb) % LANES
        acc = zero
        for s in range(sb):
            rr = i * sb + s
            acc = jnp.where(lane == col0 + s, red[rr * sb:(rr + 1) * sb, :], acc)
        keep = (lane <= trow + col0) if fwd else (lane >= trow + col0)
        acc = jnp.where(keep, acc, 0.0)
        tiles = [zero] * (n * c // LANES)
        tiles[(i * sb) // LANES] = acc
        diag_rows.append(jnp.concatenate(tiles, axis=1))
    a = a + jnp.concatenate(diag_rows, axis=0)

    av = _dot(a.astype(BF16), v)
    return q_in, k_dec, av, e_tot


def _gla_kernel(qf, kf, vf, lrf, qb, kb, vb, lrb, wd_ref, bd_ref, tri_ref, lvl_ref,
                of_ref, ob_ref, stf, stb, ksf, bsf, dbf, ksb, bsb, dbb):
    c, n = GLA_CHUNK, GLA_CHUNKS_PER_STEP

    @pl.when(pl.program_id(2) == 0)
    def _init():
        stf[...] = jnp.zeros_like(stf)
        stb[...] = jnp.zeros_like(stb)

    prep_f = _gla_prepare(qf, kf, vf, lrf, wd_ref[0], bd_ref[0], tri_ref[0], lvl_ref.at[0], ksf, bsf, dbf, True)
    prep_b = _gla_prepare(qb, kb, vb, lrb, wd_ref[1], bd_ref[1], tri_ref[1], lvl_ref.at[1], ksb, bsb, dbb, False)

    def scan(prep, v_ref, st_ref, order):
        q_in, k_dec, av, e_tot = prep
        state = st_ref[...]
        outs = [None] * n
        for ci in order:
            rows = slice(ci * c, (ci + 1) * c)
            outs[ci] = _dot_nt(q_in[rows], state.astype(BF16)) + av[rows]
            state = state * e_tot[ci] + _dot_tn(v_ref[0, rows, :], k_dec[rows])
        return jnp.concatenate(outs, axis=0), state

    o_f, s_f = scan(prep_f, vf, stf, range(n))
    o_b, s_b = scan(prep_b, vb, stb, range(n - 1, -1, -1))
    of_ref[0] = o_f
    ob_ref[0] = o_b
    stf[...] = s_f
    stb[...] = s_b


def _gla_scan(proj, lr, wd, bd):
    b, t, _ = proj.shape
    n_steps = t // ROW_TILE
    fwd = lambda si: si
    bwd = lambda si: jnp.where(si == 0, 0, n_steps - si)

    def specs(tile):
        return [
            pl.BlockSpec((1, ROW_TILE, GLA_DK), lambda bi, hi, si: (bi, tile(si), COL_GQ // GLA_DK + hi)),
            pl.BlockSpec((1, ROW_TILE, GLA_DK), lambda bi, hi, si: (bi, tile(si), COL_GK // GLA_DK + hi)),
            pl.BlockSpec((1, ROW_TILE, GLA_DV), lambda bi, hi, si: (bi, tile(si), COL_GV // GLA_DV + hi)),
            pl.BlockSpec((1, ROW_TILE, LANES), lambda bi, hi, si: (bi, tile(si), 0)),
        ]

    out_spec = lambda tile: pl.BlockSpec((1, ROW_TILE, GLA_DV), lambda bi, hi, si: (bi, tile(si), hi))
    tri, lvl = _gla_masks()
    tile_scratch = [
        pltpu.VMEM((ROW_TILE, GLA_DK), F32), pltpu.VMEM((ROW_TILE, GLA_DK), F32),
        pltpu.VMEM((ROW_TILE * GLA_SUB, GLA_DK), BF16),
    ]
    return pl.pallas_call(
        _gla_kernel,
        grid=(b, GLA_HEADS, n_steps),
        in_specs=specs(fwd) + specs(bwd) + [
            pl.BlockSpec((2, LANES, GLA_DK), lambda bi, hi, si: (0, 0, hi)),
            pl.BlockSpec((2, 1, GLA_DK), lambda bi, hi, si: (0, 0, hi)),
            pl.BlockSpec(tri.shape, lambda bi, hi, si: (0, 0, 0)),
            pl.BlockSpec(lvl.shape, lambda bi, hi, si: (0, 0, 0, 0)),
        ],
        out_specs=[out_spec(fwd), out_spec(bwd)],
        out_shape=[jax.ShapeDtypeStruct((b, t, GLA_WIDTH), F32)] * 2,
        scratch_shapes=[pltpu.VMEM((GLA_DV, GLA_DK), F32), pltpu.VMEM((GLA_DV, GLA_DK), F32)]
        + tile_scratch + tile_scratch,
        compiler_params=_cparams(("arbitrary", "arbitrary", "arbitrary")),
        name="gla_scan",
    )(proj, proj, proj, lr, proj, proj, proj, lr, wd, bd, tri, lvl)


def _merge_kernel(of_ref, ob_ref, gg_ref, g_ref, o_ref):
    o = of_ref[0] + ob_ref[0]
    y = o * lax.rsqrt(jnp.mean(o * o, axis=-1, keepdims=True) + EPS) * g_ref[...]
    o_ref[0] = (y * _silu(gg_ref[0].astype(F32))).astype(o_ref.dtype)


def _gla_merge(o_f, o_b, proj, gla_g):
    b, t, _ = o_f.shape
    tile = lambda bi, ti, hi: (bi, ti, hi)
    return pl.pallas_call(
        _merge_kernel,
        grid=(b, t // ROW_TILE, GLA_HEADS),
        in_specs=[
            pl.BlockSpec((1, ROW_TILE, GLA_DV), tile),
            pl.BlockSpec((1, ROW_TILE, GLA_DV), tile),
            pl.BlockSpec((1, ROW_TILE, GLA_DV), lambda bi, ti, hi: (bi, ti, COL_GG // GLA_DV + hi)),
            pl.BlockSpec((1, GLA_DV), lambda bi, ti, hi: (0, 0)),
        ],
        out_specs=pl.BlockSpec((1, ROW_TILE, GLA_DV), tile),
        out_shape=jax.ShapeDtypeStruct((b, t, GLA_WIDTH), BF16),
        compiler_params=_cparams(("arbitrary", "arbitrary", "arbitrary")),
        name="gla_merge",
    )(o_f, o_b, proj, gla_g.reshape(1, GLA_DV))


def _outproj_kernel(na_ref, gl_ref, wa_ref, wb_ref, x_ref, gate_ref, o_ref):
    y = _dot(na_ref[...], wa_ref[0, 0].astype(BF16)) + _dot(gl_ref[...], wb_ref[0, 0].astype(BF16))
    row = lax.broadcasted_iota(jnp.int32, y.shape, 0)
    first_tile = (pl.program_id(0) % MM_TILES_PER_BATCH) == 0
    is_ctx = jnp.logical_and(first_tile, row < CTX_LEN)
    gate = jnp.where(is_ctx, gate_ref[0, 1:2, :], gate_ref[0, 0:1, :])
    o_ref[...] = x_ref[...] + gate * y


def _outproj(na, gl, w_out, layer, xs2d, gates):
    m, d = xs2d.shape
    half = na.shape[1]
    return pl.pallas_call(
        _outproj_kernel,
        grid=(m // MM_TILE_M, d // MM_TILE_N),
        in_specs=[
            pl.BlockSpec((MM_TILE_M, half), lambda i, j: (i, 0)),
            pl.BlockSpec((MM_TILE_M, half), lambda i, j: (i, 0)),
            pl.BlockSpec((1, 1, half, MM_TILE_N), lambda i, j: (layer, 0, 0, j)),
            pl.BlockSpec((1, 1, half, MM_TILE_N), lambda i, j: (layer, 1, 0, j)),
            pl.BlockSpec((MM_TILE_M, MM_TILE_N), lambda i, j: (i, j)),
            pl.BlockSpec((1, 2, MM_TILE_N), lambda i, j: (i // MM_TILES_PER_BATCH, 0, j)),
        ],
        out_specs=pl.BlockSpec((MM_TILE_M, MM_TILE_N), lambda i, j: (i, j)),
        out_shape=jax.ShapeDtypeStruct((m, d), F32),
        input_output_aliases={4: 0},
        compiler_params=_cparams(("arbitrary", "arbitrary")),
        name="outproj_residual",
    )(na, gl, w_out, w_out, xs2d, gates)


def _final_norm_kernel(x_ref, g_ref, o_ref):
    x = x_ref[0]
    o_ref[0] = x * lax.rsqrt(jnp.mean(x * x, axis=-1, keepdims=True) + EPS) * g_ref[...]


def _final_norm(xs, g, n_latent):
    b, _, d = xs.shape
    skip = CTX_LEN // ROW_TILE
    return pl.pallas_call(
        _final_norm_kernel,
        grid=(b, n_latent // ROW_TILE),
        in_specs=[
            pl.BlockSpec((1, ROW_TILE, d), lambda bi, ti: (bi, ti + skip, 0)),
            pl.BlockSpec((1, d), lambda bi, ti: (0, 0)),
        ],
        out_specs=pl.BlockSpec((1, ROW_TILE, d), lambda bi, ti: (bi, ti, 0)),
        out_shape=jax.ShapeDtypeStruct((b, n_latent, d), F32),
        compiler_params=_cparams(("arbitrary", "arbitrary")),
        name="final_norm",
    )(xs, g.reshape(1, d))


def kernel(x, c, ctx, c_ctx, ada_w, ada_b, norm_g, w_in, na_rpb, gla_w_decay, gla_b_decay, gla_norm_g,
           w_out, final_norm_g):
    b, n_latent, d = x.shape
    depth = ada_w.shape[0]
    assert (d, n_latent, ctx.shape[1]) == (D_MODEL, GRID_W * GRID_W, CTX_LEN) and b <= 2
    t = CTX_LEN + n_latent

    cc = jnp.zeros((8, d), F32).at[:b].set(c).at[2].set(c_ctx)
    mods = _ada_modulation(cc, ada_w, ada_b).reshape(depth, 8, 3, d)

    cos, sin = _rope_tables(n_latent)
    xs = jnp.concatenate([ctx, x], axis=1)
    w_out_halves = w_out.reshape(depth, 2, NA_WIDTH, d)

    for l in range(depth):
        w_lr = jnp.pad(w_in[l, :, PROJ_WIDE:], ((0, 0), (0, LANES - 2 * GLA_GATE_RANK))).astype(BF16)
        wd = jnp.zeros((2, LANES, GLA_KEY_WIDTH), F32)
        wd = wd.at[0, :GLA_GATE_RANK].set(gla_w_decay[l, 0])
        wd = wd.at[1, GLA_GATE_RANK:2 * GLA_GATE_RANK].set(gla_w_decay[l, 1]).astype(BF16)
        bd = gla_b_decay[l].reshape(2, 1, GLA_KEY_WIDTH)
        gates = jnp.stack([mods[l, :b, 2], jnp.broadcast_to(mods[l, 2, 2], (b, d))], axis=1)

        h, lr = _prenorm(xs, norm_g[l], mods[l], w_lr)
        proj = _inproj(h.reshape(b * t, d), w_in, l, cos, sin).reshape(b, t, PROJ_WIDE)
        na = _na_attention(proj, _na_bias_table(na_rpb[l]))
        o_f, o_b = _gla_scan(proj, lr, wd, bd)
        gl = _gla_merge(o_f, o_b, proj, gla_norm_g[l])
        xs = _outproj(na.reshape(b * t, NA_WIDTH), gl.reshape(b * t, GLA_WIDTH), w_out_halves, l,
                      xs.reshape(b * t, d), gates).reshape(b, t, d)

    return _final_norm(xs, final_norm_g, n_latent)
```
